```python
import jax, jax.numpy as jnp
from jax import lax
import numpy as np

D_MODEL = 1024
BATCH = 2
SEQ = 8192
DEPTH = 1
DEC_BATCH = 8
DEC_SEQ = 16
PAST_LEN = 2048

CHUNK = 64
N_META = 16
POOL_WIDTH = 512
POOL_WINDOWS = (2, 4, 8, 16)
N_POOL_GROUPS = 4
POOL_GROUP = POOL_WIDTH // N_POOL_GROUPS
POOL_BUF = max(POOL_WINDOWS) - 1
GLA_HEADS = 4
GLA_DK = 128
GLA_DV = 256
GLA_KW = GLA_HEADS * GLA_DK
GLA_VW = GLA_HEADS * GLA_DV
GATE_RANK = 16
GATE_TAU = 16.0
N_BRANCH = 2
IN_WIDTH = POOL_WIDTH + 2 * GLA_KW + 2 * GLA_VW + GATE_RANK + N_BRANCH * D_MODEL
PEER_HEADS = 8
PEER_NKEYS = 128
PEER_N = PEER_NKEYS * PEER_NKEYS
PEER_HALF = 128
PEER_TOPK = 16
PEER_BLOCK = 256
ALPHA = (2 * DEPTH) ** 0.25
BETA = (8 * DEPTH) ** -0.25
LN_EPS = 1e-5
SPLITS = [int(s) for s in np.cumsum([POOL_WIDTH, GLA_KW, GLA_KW, GLA_VW, GLA_VW, GATE_RANK])]

kernel_name = "hybrid_pool_gla_peer_stream_step"


def layer_norm(x, g, b):
    xf = x.astype(jnp.float32)
    mu = jnp.mean(xf, -1, keepdims=True)
    var = jnp.mean(jnp.square(xf - mu), -1, keepdims=True)
    return ((xf - mu) * lax.rsqrt(var + LN_EPS) * g.astype(jnp.float32) + b.astype(jnp.float32)).astype(x.dtype)


def pool_mix(u, prev, n_valid_prev):
    B, L, P = u.shape
    ext = jnp.concatenate([prev, u], axis=1).astype(jnp.float32)
    c = jnp.concatenate([jnp.zeros((B, 1, P), jnp.float32), jnp.cumsum(ext, axis=1)], axis=1)
    t = jnp.arange(L)
    outs = []
    for gi, w in enumerate(POOL_WINDOWS):
        cg = c[..., gi * POOL_GROUP:(gi + 1) * POOL_GROUP]
        hi = cg[:, POOL_BUF + 1:POOL_BUF + 1 + L]
        lo = cg[:, POOL_BUF + 1 - w:POOL_BUF + 1 - w + L]
        cnt = jnp.minimum(w, t + 1 + n_valid_prev).astype(jnp.float32)
        outs.append((hi - lo) / cnt[None, :, None])
    return jnp.concatenate(outs, axis=-1) - ext[:, POOL_BUF:]


def gla_block(q, k, v, lg, S):
    L = q.shape[2]
    b = jnp.cumsum(lg, axis=2)
    causal = jnp.tril(jnp.ones((L, L), bool))
    diff = b[:, :, :, None, :] - b[:, :, None, :, :]
    decay = jnp.exp(jnp.where(causal[:, :, None], diff, -jnp.inf))
    scores = jnp.einsum('bhtk,bhsk,bhtsk->bhts', q, k, decay)
    o = jnp.einsum('bhts,bhsv->bhtv', scores, v) + jnp.einsum('bhtk,bhkv->bhtv', q * jnp.exp(b), S)
    b_last = b[:, :, -1]
    k_dec = k * jnp.exp(b_last[:, :, None, :] - b)
    S_new = jnp.exp(b_last)[..., None] * S + jnp.einsum('bhsk,bhsv->bhkv', k_dec, v)
    return o, S_new


def gla_sequence(q, k, v, lg, S0, n_lead):
    o0, S = gla_block(q[:, :, :n_lead], k[:, :, :n_lead], v[:, :, :n_lead], lg[:, :, :n_lead], S0)
    rest = q.shape[2] - n_lead
    if rest == 0:
        return o0, S
    nc = rest // CHUNK

    def to_chunks(a):
        B, H, _, d = a.shape
        return jnp.moveaxis(a[:, :, n_lead:].reshape(B, H, nc, CHUNK, d), 2, 0)

    def step(S, blk):
        o, S = gla_block(blk[0], blk[1], blk[2], blk[3], S)
        return S, o

    S, oc = lax.scan(step, S, (to_chunks(q), to_chunks(k), to_chunks(v), to_chunks(lg)))
    B, H = q.shape[0], q.shape[1]
    oc = jnp.moveaxis(oc, 0, 2).reshape(B, H, rest, GLA_DV)
    return jnp.concatenate([o0, oc], axis=2), S


def token_mixer(h, pool_prev, n_valid_prev, S0, n_lead, w_in, w_f2, b_f, w_pool_group, pool_scale,
                w_pool_out, gla_norm_g, w_gla_out, w_o):
    B, L, _ = h.shape
    f32 = jnp.float32
    z = h @ w_in
    u, q, k, v, r, f, gates = jnp.split(z, SPLITS, axis=-1)
    d = pool_mix(u, pool_prev, n_valid_prev)
    d = jnp.einsum('blgc,gcd->blgd', d.reshape(B, L, N_POOL_GROUPS, POOL_GROUP),
                   w_pool_group.astype(f32)).reshape(B, L, POOL_WIDTH)
    y_pool = (d * pool_scale.astype(f32)).astype(h.dtype) @ w_pool_out
    lg = jax.nn.log_sigmoid((f @ w_f2 + b_f).astype(f32)) / GATE_TAU

    def heads(a, dh):
        return a.astype(f32).reshape(B, L, GLA_HEADS, dh).transpose(0, 2, 1, 3)

    o, S = gla_sequence(heads(q, GLA_DK) * (GLA_DK ** -0.5), heads(k, GLA_DK), heads(v, GLA_DV),
                        heads(lg, GLA_DK), S0.astype(f32), n_lead)
    o = o * lax.rsqrt(jnp.mean(jnp.square(o), -1, keepdims=True) + LN_EPS)
    o = o.transpose(0, 2, 1, 3).reshape(B, L, GLA_VW) * gla_norm_g.astype(f32) * jax.nn.silu(r.astype(f32))
    y_gla = o.astype(h.dtype) @ w_gla_out
    g_pool, g_gla = jnp.split(jax.nn.sigmoid(gates.astype(f32)), N_BRANCH, axis=-1)
    m = g_pool * y_pool.astype(f32) + g_gla * y_gla.astype(f32)
    out = m.astype(h.dtype) @ w_o
    new_pool = jnp.concatenate([pool_prev, u], axis=1)[:, -POOL_BUF:]
    return out, new_pool, S


def peer(h, w_peer_q, peer_sub_keys, expert_u, expert_v):
    B, L, D = h.shape
    f32 = jnp.float32
    n = B * L
    nb = -(-n // PEER_BLOCK)
    xp = jnp.pad(h.reshape(n, D), ((0, nb * PEER_BLOCK - n), (0, 0))).reshape(nb, PEER_BLOCK, D)

    def block(xb):
        T = xb.shape[0]
        q = (xb @ w_peer_q).astype(f32).reshape(T, PEER_HEADS, 2, PEER_HALF)
        s = jnp.einsum('thpc,hpnc->thpn', q, peer_sub_keys.astype(f32))
        s1, i1 = lax.top_k(s[:, :, 0], PEER_TOPK)
        s2, i2 = lax.top_k(s[:, :, 1], PEER_TOPK)
        cand = (s1[..., :, None] + s2[..., None, :]).reshape(T, PEER_HEADS, PEER_TOPK * PEER_TOPK)
        cidx = (i1[..., :, None] * PEER_NKEYS + i2[..., None, :]).reshape(T, PEER_HEADS, PEER_TOPK * PEER_TOPK)
        top, pos = lax.top_k(cand, PEER_TOPK)
        eidx = jnp.take_along_axis(cidx, pos, axis=-1)
        gate = jax.nn.softmax(top, axis=-1)
        ue = jnp.take(expert_u, eidx, axis=0).astype(f32)
        a = jax.nn.gelu(jnp.einsum('thkd,td->thk', ue, xb.astype(f32)), approximate=False)
        ve = jnp.take(expert_v, eidx, axis=0).astype(f32)
        return jnp.einsum('thk,thkd->td', gate * a, ve).astype(xb.dtype)

    out = lax.map(block, xp).reshape(nb * PEER_BLOCK, D)[:n]
    return out.reshape(B, L, D)


def encoder_layer(h, pool_prev, n_valid_prev, S0, n_lead, lw):
    (w_in, w_f2, b_f, w_pool_group, pool_scale, w_pool_out, gla_norm_g, w_gla_out, w_o,
     ln1_g, ln1_b, w_peer_q, peer_sub_keys, expert_u, expert_v, ln2_g, ln2_b) = lw
    out, new_pool, S = token_mixer(h, pool_prev, n_valid_prev, S0, n_lead, w_in, w_f2, b_f, w_pool_group,
                                   pool_scale, w_pool_out, gla_norm_g, w_gla_out, w_o)
    h = layer_norm(ALPHA * h + out, ln1_g, ln1_b)
    h = layer_norm(ALPHA * h + peer(h, w_peer_q, peer_sub_keys, expert_u, expert_v), ln2_g, ln2_b)
    return h, new_pool, S


def setup_inputs(seed: int = 0) -> dict:
    key = jax.random.key(seed)
    ks = jax.random.split(key, 24)
    nrm = jax.random.normal
    f32 = jnp.float32
    D = D_MODEL
    return {
        "x_prompt": nrm(ks[0], (BATCH, SEQ, D), f32),
        "x_sample": nrm(ks[1], (DEC_BATCH, DEC_SEQ, D), f32),
        "state_pool": nrm(ks[2], (DEPTH, DEC_BATCH, POOL_BUF, POOL_WIDTH), f32),
        "state_gla": nrm(ks[3], (DEPTH, DEC_BATCH, GLA_HEADS, GLA_DK, GLA_DV), f32),
        "meta_tokens": nrm(ks[4], (N_META, D), f32),
        "ln_in_g": 1.0 + 0.02 * nrm(ks[5], (D,), f32),
        "ln_in_b": 0.02 * nrm(ks[6], (D,), f32),
        "w_in": nrm(ks[7], (DEPTH, D, IN_WIDTH), f32) * D ** -0.5,
        "w_f2": nrm(ks[8], (DEPTH, GATE_RANK, GLA_KW), f32) * GATE_RANK ** -0.5,
        "b_f": 0.1 * nrm(ks[9], (DEPTH, GLA_KW), f32),
        "w_pool_group": nrm(ks[10], (DEPTH, N_POOL_GROUPS, POOL_GROUP, POOL_GROUP), f32) * POOL_GROUP ** -0.5,
        "pool_scale": 1.0 + 0.02 * nrm(ks[11], (DEPTH, POOL_WIDTH), f32),
        "w_pool_out": nrm(ks[12], (DEPTH, POOL_WIDTH, D), f32) * POOL_WIDTH ** -0.5,
        "gla_norm_g": 1.0 + 0.02 * nrm(ks[13], (DEPTH, GLA_VW), f32),
        "w_gla_out": nrm(ks[14], (DEPTH, GLA_VW, D), f32) * GLA_VW ** -0.5,
        "w_o": nrm(ks[15], (DEPTH, D, D), f32) * (D ** -0.5 * BETA),
        "ln1_g": 1.0 + 0.02 * nrm(ks[16], (DEPTH, D), f32),
        "ln1_b": 0.02 * nrm(ks[17], (DEPTH, D), f32),
        "w_peer_q": nrm(ks[18], (DEPTH, D, PEER_HEADS * 2 * PEER_HALF), f32) * D ** -0.5,
        "peer_sub_keys": nrm(ks[19], (DEPTH, PEER_HEADS, 2, PEER_NKEYS, PEER_HALF), f32) * PEER_HALF ** -0.5,
        "expert_u": nrm(ks[20], (DEPTH, PEER_N, D), f32) * D ** -0.5,
        "expert_v": nrm(ks[21], (DEPTH, PEER_N, D), f32) * BETA,
        "ln2_g": 1.0 + 0.02 * nrm(ks[22], (DEPTH, D), f32),
        "ln2_b": 0.02 * nrm(ks[23], (DEPTH, D), f32),
    }


def reference(x_prompt, x_sample, state_pool, state_gla, meta_tokens, ln_in_g, ln_in_b, w_in, w_f2, b_f,
              w_pool_group, pool_scale, w_pool_out, gla_norm_g, w_gla_out, w_o, ln1_g, ln1_b, w_peer_q,
              peer_sub_keys, expert_u, expert_v, ln2_g, ln2_b):
    B = x_prompt.shape[0]
    meta = jnp.broadcast_to(meta_tokens.astype(x_prompt.dtype)[None], (B, N_META, D_MODEL))
    h_p = layer_norm(jnp.concatenate([meta, x_prompt], axis=1), ln_in_g, ln_in_b)
    h_s = layer_norm(x_sample, ln_in_g, ln_in_b)
    pool_p, gla_p, pool_s, gla_s = [], [], [], []
    for l in range(DEPTH):
        lw = (w_in[l], w_f2[l], b_f[l], w_pool_group[l], pool_scale[l], w_pool_out[l], gla_norm_g[l],
              w_gla_out[l], w_o[l], ln1_g[l], ln1_b[l], w_peer_q[l], peer_sub_keys[l], expert_u[l],
              expert_v[l], ln2_g[l], ln2_b[l])
        zero_pool = jnp.zeros((B, POOL_BUF, POOL_WIDTH), h_p.dtype)
        zero_S = jnp.zeros((B, GLA_HEADS, GLA_DK, GLA_DV), jnp.float32)
        h_p, np_p, S_p = encoder_layer(h_p, zero_pool, 0, zero_S, N_META, lw)
        h_s, np_s, S_s = encoder_layer(h_s, state_pool[l].astype(h_s.dtype), POOL_BUF, state_gla[l],
                                       h_s.shape[1], lw)
        pool_p.append(np_p)
        gla_p.append(S_p.astype(x_prompt.dtype))
        pool_s.append(np_s)
        gla_s.append(S_s.astype(x_sample.dtype))
    y_prompt = h_p[:, N_META:]
    y_sample = h_s
    new_pool_prompt = jnp.stack(pool_p)
    new_gla_prompt = jnp.stack(gla_p)
    new_pool_sample = jnp.stack(pool_s)
    new_gla_sample = jnp.stack(gla_s)
    return (y_prompt, y_sample, new_pool_prompt, new_gla_prompt, new_pool_sample, new_gla_sample)
```

```python
import functools

import jax
import jax.numpy as jnp
from jax import lax
from jax.experimental import pallas as pl
from jax.experimental.pallas import tpu as pltpu

D_MODEL = 1024
DEPTH = 1
N_META = 16
CHUNK = 64
POOL_WIDTH = 512
POOL_WINDOWS = (2, 4, 8, 16)
POOL_GROUP = 128
POOL_BUF = 15
POOL_TAIL = 16
GLA_HEADS = 4
GLA_DK = 128
GLA_DV = 256
GLA_KW = GLA_HEADS * GLA_DK
GLA_VW = GLA_HEADS * GLA_DV
GATE_RANK = 16
GATE_TAU = 16.0
PEER_HEADS = 8
PEER_NKEYS = 128
PEER_N = PEER_NKEYS * PEER_NKEYS
PEER_HALF = 128
PEER_TOPK = 16
ALPHA = (2 * DEPTH) ** 0.25
LN_EPS = 1e-5
MAIN_W = POOL_WIDTH + 2 * GLA_KW + 2 * GLA_VW
LANE = 128
VMEM_LIMIT = 56 * 1024 * 1024

MIXER_TM = 256
PEER_A_TM = 256
PEER_B_TM = 1024
PEER_B_RT = 256
PEER_B_EB = 512

NT = (((1,), (1,)), ((), ()))
TN = (((0,), (0,)), ((), ()))


def _ln(x, g, b):
    mu = jnp.mean(x, axis=-1, keepdims=True)
    xc = x - mu
    var = jnp.mean(xc * xc, axis=-1, keepdims=True)
    return xc * lax.rsqrt(var + LN_EPS) * g + b


def _dot(a, b):
    return jnp.dot(a, b, preferred_element_type=jnp.float32)


def _bf(x):
    return x.astype(jnp.bfloat16)


def _mixer_kernel(x_ref, icnt_ref, pool0_ref, s0_ref, lng_ref, lnb_ref, wmain_ref, wf_ref, wgate_ref,
                  wf2_ref, bf_ref, wpg_ref, pscale_ref, wpo_ref, gng_ref, wgo_ref, wo_ref, ln1g_ref,
                  ln1b_ref, h1_ref, h1b_ref, poolo_ref, so_ref, ext_ref, st_ref, o_ref, *, tm, ch):
    t = pl.program_id(1)
    nt = pl.num_programs(1)

    @pl.when(t == 0)
    def _():
        ext_ref[0:POOL_TAIL, :] = pool0_ref[0]
        for hd in range(GLA_HEADS):
            st_ref[hd] = s0_ref[0, hd].T

    h = _ln(x_ref[0], lng_ref[...], lnb_ref[...])
    hb = _bf(h)
    zm = _dot(hb, wmain_ref[...])
    u = zm[:, 0:POOL_WIDTH]
    q = zm[:, POOL_WIDTH:POOL_WIDTH + GLA_KW]
    k = zm[:, POOL_WIDTH + GLA_KW:POOL_WIDTH + 2 * GLA_KW]
    v = zm[:, POOL_WIDTH + 2 * GLA_KW:POOL_WIDTH + 2 * GLA_KW + GLA_VW]
    r = zm[:, POOL_WIDTH + 2 * GLA_KW + GLA_VW:MAIN_W]
    f = _dot(hb, wf_ref[...])
    gates = _dot(hb, wgate_ref[...])

    ext_ref[POOL_TAIL:POOL_TAIL + tm, :] = u
    icnt = icnt_ref[0]
    dparts = []
    for gi, w in enumerate(POOL_WINDOWS):
        cols = slice(gi * POOL_GROUP, (gi + 1) * POOL_GROUP)
        acc = u[:, cols]
        for s in range(1, w):
            acc = acc + ext_ref[POOL_TAIL - s:POOL_TAIL - s + tm, cols]
        dg = acc * icnt[:, cols] - u[:, cols]
        dparts.append(_dot(_bf(dg), wpg_ref[gi]))
    d = jnp.concatenate(dparts, axis=-1) * pscale_ref[...]
    y_pool = _dot(_bf(d), wpo_ref[...])
    new_tail = ext_ref[tm:tm + POOL_TAIL, :]
    ext_ref[0:POOL_TAIL, :] = new_tail

    glog = _dot(_bf(f), wf2_ref[...]) + bf_ref[...]
    lg = (jnp.minimum(glog, 0.0) - jnp.log1p(jnp.exp(-jnp.abs(glog)))) * (1.0 / GATE_TAU)
    ri = lax.broadcasted_iota(jnp.int32, (tm, tm), 0)
    ci = lax.broadcasted_iota(jnp.int32, (tm, tm), 1)
    ltri = jnp.where(ci <= ri, jnp.where(ci >= (ri // ch) * ch, 1.0, 0.0), 0.0).astype(jnp.bfloat16)
    lg_hi = _bf(lg)
    rem = lg - lg_hi.astype(jnp.float32)
    lg_mid = _bf(rem)
    lg_lo = _bf(rem - lg_mid.astype(jnp.float32))
    bcum = _dot(ltri, lg_hi) + _dot(ltri, lg_mid) + _dot(ltri, lg_lo)
    causal = (lax.broadcasted_iota(jnp.int32, (ch, ch), 1) <= lax.broadcasted_iota(jnp.int32, (ch, ch), 0))
    scale = GLA_DK ** -0.5
    for c in range(tm // ch):
        rows = slice(c * ch, (c + 1) * ch)
        bc = bcum[rows]
        bl = bc[ch - 1:ch]
        qc = q[rows] * scale
        qe = _bf(qc * jnp.exp(bc))
        qi = _bf(qc * jnp.exp(bc - bl))
        kd = _bf(k[rows] * jnp.exp(bl - bc))
        dl = jnp.exp(bl)
        for hd in range(GLA_HEADS):
            kc = slice(hd * GLA_DK, (hd + 1) * GLA_DK)
            vb = _bf(v[rows, hd * GLA_DV:(hd + 1) * GLA_DV])
            st = st_ref[hd]
            sc = lax.dot_general(qi[:, kc], kd[:, kc], NT, preferred_element_type=jnp.float32)
            sc = jnp.where(causal, sc, 0.0)
            o = _dot(_bf(sc), vb) + lax.dot_general(qe[:, kc], _bf(st), NT,
                                                    preferred_element_type=jnp.float32)
            o_ref[rows, hd * GLA_DV:(hd + 1) * GLA_DV] = o
            st_ref[hd] = st * dl[:, kc] + lax.dot_general(vb, kd[:, kc], TN,
                                                          preferred_element_type=jnp.float32)
    oparts = []
    for hd in range(GLA_HEADS):
        oh = o_ref[:, hd * GLA_DV:(hd + 1) * GLA_DV]
        oparts.append(oh * lax.rsqrt(jnp.mean(oh * oh, axis=-1, keepdims=True) + LN_EPS))
    og = jnp.concatenate(oparts, axis=-1) * gng_ref[...] * (r * jax.nn.sigmoid(r))
    y_gla = _dot(_bf(og), wgo_ref[...])

    m = jax.nn.sigmoid(gates[:, 0:D_MODEL]) * y_pool + jax.nn.sigmoid(gates[:, D_MODEL:]) * y_gla
    out = _dot(_bf(m), wo_ref[...])
    h1 = _ln(ALPHA * h + out, ln1g_ref[...], ln1b_ref[...])
    h1_ref[0] = h1
    h1b_ref[0] = _bf(h1)

    @pl.when(t == nt - 1)
    def _():
        poolo_ref[0] = new_tail
        for hd in range(GLA_HEADS):
            so_ref[0, hd] = st_ref[hd].T


def _const_spec(shape):
    nd = len(shape)
    return pl.BlockSpec(shape, lambda *_: (0,) * nd, pipeline_mode=pl.Buffered(1))


def _mixer(x, icnt, icnt_per_seq, pool0, s0, wts, *, tm, ch):
    n_seq, L, _ = x.shape
    nt = L // tm
    assert nt * tm == L and tm % ch == 0
    if icnt_per_seq:
        icnt_spec = pl.BlockSpec((1, tm, POOL_WIDTH), lambda s, t: (s, t, 0))
    else:
        icnt_spec = pl.BlockSpec((1, tm, POOL_WIDTH), lambda s, t: (0, 0, 0))
    in_specs = [
        pl.BlockSpec((1, tm, D_MODEL), lambda s, t: (s, t, 0)),
        icnt_spec,
        pl.BlockSpec((1, POOL_TAIL, POOL_WIDTH), lambda s, t: (s, 0, 0)),
        pl.BlockSpec((1, GLA_HEADS, GLA_DK, GLA_DV), lambda s, t: (s, 0, 0, 0)),
    ] + [_const_spec(w.shape) for w in wts]
    out_shape = (
        jax.ShapeDtypeStruct((n_seq, L, D_MODEL), jnp.float32),
        jax.ShapeDtypeStruct((n_seq, L, D_MODEL), jnp.bfloat16),
        jax.ShapeDtypeStruct((n_seq, POOL_TAIL, POOL_WIDTH), jnp.float32),
        jax.ShapeDtypeStruct((n_seq, GLA_HEADS, GLA_DK, GLA_DV), jnp.float32),
    )
    out_specs = (
        pl.BlockSpec((1, tm, D_MODEL), lambda s, t: (s, t, 0)),
        pl.BlockSpec((1, tm, D_MODEL), lambda s, t: (s, t, 0)),
        pl.BlockSpec((1, POOL_TAIL, POOL_WIDTH), lambda s, t: (s, 0, 0)),
        pl.BlockSpec((1, GLA_HEADS, GLA_DK, GLA_DV), lambda s, t: (s, 0, 0, 0)),
    )
    return pl.pallas_call(
        functools.partial(_mixer_kernel, tm=tm, ch=ch),
        grid=(n_seq, nt),
        in_specs=in_specs,
        out_specs=out_specs,
        out_shape=out_shape,
        scratch_shapes=[
            pltpu.VMEM((POOL_TAIL + tm, POOL_WIDTH), jnp.float32),
            pltpu.VMEM((GLA_HEADS, GLA_DV, GLA_DK), jnp.float32),
            pltpu.VMEM((tm, GLA_VW), jnp.float32),
        ],
        compiler_params=pltpu.CompilerParams(dimension_semantics=("arbitrary", "arbitrary"),
                                             vmem_limit_bytes=VMEM_LIMIT),
        name=f"mixer_tm{tm}",
    )(x, icnt, pool0, s0, *wts)


N_EXTRACT = PEER_TOPK + 1


def _top_rows(x, n):
    vals = []
    cur = x
    for i in range(n):
        m = jnp.max(cur, axis=0, keepdims=True)
        vals.append(m)
        if i + 1 < n:
            cur = jnp.where(cur == m, -jnp.inf, cur)
    return vals


def _peer_a_kernel(hb_ref, wq_ref, keys_ref, qb_ref, tau_ref, lam_ref):
    tm = hb_ref.shape[0]
    qb = _bf(_dot(hb_ref[...], wq_ref[...]))
    qb_ref[...] = qb
    taus, lams = [], []
    for hd in range(PEER_HEADS):
        tops = []
        for p in range(2):
            j = hd * 2 + p
            s_t = lax.dot_general(keys_ref[j], qb[:, j * PEER_HALF:(j + 1) * PEER_HALF], NT,
                                  preferred_element_type=jnp.float32)
            tops.append(_top_rows(s_t, N_EXTRACT))
        a, b = tops
        rows = [a[i] + b[j] for i in range(N_EXTRACT) for j in range(N_EXTRACT)
                if (i + 1) * (j + 1) <= N_EXTRACT]
        pad = (-len(rows)) % 8
        rows += [jnp.full((1, tm), -jnp.inf, jnp.float32)] * pad
        c = _top_rows(jnp.concatenate(rows, axis=0), N_EXTRACT)
        z = jnp.ones_like(c[0])
        for i in range(1, PEER_TOPK):
            z = z + jnp.exp(c[i] - c[0])
        taus.append(0.5 * (c[PEER_TOPK - 1] + c[PEER_TOPK]))
        lams.append(c[0] + jnp.log(z))
    tau_ref[...] = jnp.concatenate(taus, axis=0)
    lam_ref[...] = jnp.concatenate(lams, axis=0)


def _peer_a(hb, wq, keys, *, tm):
    T = hb.shape[0]
    assert T % tm == 0
    qw = wq.shape[1]
    return pl.pallas_call(
        _peer_a_kernel,
        grid=(T // tm,),
        in_specs=[
            pl.BlockSpec((tm, D_MODEL), lambda i: (i, 0)),
            _const_spec(wq.shape),
            _const_spec(keys.shape),
        ],
        out_specs=(
            pl.BlockSpec((tm, qw), lambda i: (i, 0)),
            pl.BlockSpec((PEER_HEADS, tm), lambda i: (0, i)),
            pl.BlockSpec((PEER_HEADS, tm), lambda i: (0, i)),
        ),
        out_shape=(
            jax.ShapeDtypeStruct((T, qw), jnp.bfloat16),
            jax.ShapeDtypeStruct((PEER_HEADS, T), jnp.float32),
            jax.ShapeDtypeStruct((PEER_HEADS, T), jnp.float32),
        ),
        compiler_params=pltpu.CompilerParams(dimension_semantics=("arbitrary",),
                                             vmem_limit_bytes=VMEM_LIMIT),
        name=f"peer_a_tm{tm}",
    )(hb, wq, keys)


def _peer_b_kernel(xb_ref, qb_ref, tau_ref, lam_ref, h1_ref, u_ref, v_ref, kexp_ref, g_ref, b_ref,
                   y_ref, acc_ref, *, rt):
    j = pl.program_id(1)
    tm = xb_ref.shape[0]
    qh = 2 * PEER_HALF

    @pl.when(j == 0)
    def _():
        acc_ref[...] = jnp.zeros_like(acc_ref)

    def sub(ri, carry):
        rows = pl.ds(pl.multiple_of(ri * rt, rt), rt)
        a = lax.dot_general(xb_ref[rows, :], u_ref[...], NT, preferred_element_type=jnp.float32)
        tau = tau_ref[rows, :]
        lam = lam_ref[rows, :]
        g = jnp.zeros_like(a)
        for hd in range(PEER_HEADS):
            s = _dot(qb_ref[rows, hd * qh:(hd + 1) * qh], kexp_ref[hd])
            g = g + jnp.where(s >= tau[:, hd:hd + 1], jnp.exp(s - lam[:, hd:hd + 1]), 0.0)
        hm = _bf(g * (0.5 * a * (1.0 + lax.erf(a * (2.0 ** -0.5)))))
        acc_ref[rows, :] += _dot(hm, v_ref[...])
        return carry

    lax.fori_loop(0, tm // rt, sub, 0)

    @pl.when(j == pl.num_programs(1) - 1)
    def _():
        y_ref[...] = _ln(ALPHA * h1_ref[...] + acc_ref[...], g_ref[...], b_ref[...])


def _peer_b(xb, qb, tau_t, lam_t, h1, ub, vb, kexp, g, b, *, tm, rt, eb):
    T = xb.shape[0]
    assert T % tm == 0 and tm % rt == 0 and PEER_N % eb == 0
    qw = qb.shape[1]
    return pl.pallas_call(
        functools.partial(_peer_b_kernel, rt=rt),
        grid=(T // tm, PEER_N // eb),
        in_specs=[
            pl.BlockSpec((tm, D_MODEL), lambda i, j: (i, 0)),
            pl.BlockSpec((tm, qw), lambda i, j: (i, 0)),
            pl.BlockSpec((tm, PEER_HEADS), lambda i, j: (i, 0)),
            pl.BlockSpec((tm, PEER_HEADS), lambda i, j: (i, 0)),
            pl.BlockSpec((tm, D_MODEL), lambda i, j: (i, 0)),
            pl.BlockSpec((eb, D_MODEL), lambda i, j: (j, 0)),
            pl.BlockSpec((eb, D_MODEL), lambda i, j: (j, 0)),
            pl.BlockSpec((PEER_HEADS, 2 * PEER_HALF, eb), lambda i, j: (0, 0, j)),
            pl.BlockSpec((1, D_MODEL), lambda i, j: (0, 0)),
            pl.BlockSpec((1, D_MODEL), lambda i, j: (0, 0)),
        ],
        out_specs=pl.BlockSpec((tm, D_MODEL), lambda i, j: (i, 0)),
        out_shape=jax.ShapeDtypeStruct((T, D_MODEL), jnp.float32),
        scratch_shapes=[pltpu.VMEM((tm, D_MODEL), jnp.float32)],
        compiler_params=pltpu.CompilerParams(dimension_semantics=("arbitrary", "arbitrary"),
                                             vmem_limit_bytes=VMEM_LIMIT),
        name=f"peer_b_tm{tm}",
    )(xb, qb, tau_t, lam_t, h1, ub, vb, kexp, g, b)


def _peer(h1, h1b, wq, keys, ub, vb, kexp, g, b, *, tm_a, tm_b, rt):
    qb, tau, lam = _peer_a(h1b, wq, keys, tm=tm_a)
    return _peer_b(h1b, qb, tau.T, lam.T, h1, ub, vb, kexp, g, b, tm=tm_b, rt=rt, eb=PEER_B_EB)


def _inv_counts(n_valid_prev, length, start):
    pos = jnp.arange(start, start + length, dtype=jnp.float32)[:, None] + (1.0 + n_valid_prev)
    w = jnp.repeat(jnp.asarray(POOL_WINDOWS, jnp.float32), POOL_GROUP)[None, :]
    return 1.0 / jnp.minimum(w, pos)


def kernel(x_prompt, x_sample, state_pool, state_gla, meta_tokens, ln_in_g, ln_in_b, w_in, w_f2, b_f,
           w_pool_group, pool_scale, w_pool_out, gla_norm_g, w_gla_out, w_o, ln1_g, ln1_b, w_peer_q,
           peer_sub_keys, expert_u, expert_v, ln2_g, ln2_b):
    f32, bf16 = jnp.float32, jnp.bfloat16
    B, S, D = x_prompt.shape
    DB, DS, _ = x_sample.shape
    assert DEPTH == 1 and DS == N_META and S % MIXER_TM == 0
    row = lambda a: a.reshape(1, -1).astype(f32)

    wi = w_in[0]
    f0 = MAIN_W
    w_main = wi[:, :f0].astype(bf16)
    w_f = jnp.pad(wi[:, f0:f0 + GATE_RANK], ((0, 0), (0, LANE - GATE_RANK))).astype(bf16)
    w_gate = wi[:, f0 + GATE_RANK:].astype(bf16)
    w_f2p = jnp.pad(w_f2[0], ((0, LANE - GATE_RANK), (0, 0))).astype(bf16)
    wts = (row(ln_in_g), row(ln_in_b), w_main, w_f, w_gate, w_f2p, row(b_f[0]), w_pool_group[0].astype(bf16),
           row(pool_scale[0]), w_pool_out[0].astype(bf16), row(gla_norm_g[0]), w_gla_out[0].astype(bf16),
           w_o[0].astype(bf16), row(ln1_g[0]), row(ln1_b[0]))

    x_lead = jnp.concatenate([jnp.broadcast_to(meta_tokens.astype(f32)[None], (B, N_META, D)), x_sample], axis=0)
    pool0 = jnp.concatenate([jnp.zeros((B, POOL_TAIL, POOL_WIDTH), f32),
                             jnp.pad(state_pool[0], ((0, 0), (POOL_TAIL - POOL_BUF, 0), (0, 0)))], axis=0)
    s0 = jnp.concatenate([jnp.zeros((B, GLA_HEADS, GLA_DK, GLA_DV), f32), state_gla[0]], axis=0)
    icnt_lead = jnp.concatenate([jnp.broadcast_to(_inv_counts(0.0, N_META, 0)[None], (B, N_META, POOL_WIDTH)),
                                 jnp.broadcast_to(_inv_counts(float(POOL_BUF), N_META, 0)[None],
                                                  (DB, N_META, POOL_WIDTH))], axis=0)
    h1_lead, h1b_lead, pool_lead, s_lead = _mixer(x_lead, icnt_lead, True, pool0, s0, wts, tm=N_META, ch=N_META)

    icnt_main = _inv_counts(0.0, MIXER_TM, N_META)[None]
    h1_p, h1b_p, pool_p, s_p = _mixer(x_prompt, icnt_main, False, pool_lead[:B], s_lead[:B], wts,
                                      tm=MIXER_TM, ch=CHUNK)

    wq = w_peer_q[0].astype(bf16)
    sk = peer_sub_keys[0].astype(bf16)
    keys = sk.reshape(PEER_HEADS * 2, PEER_NKEYS, PEER_HALF)
    k1 = jnp.repeat(jnp.swapaxes(sk[:, 0], 1, 2), PEER_NKEYS, axis=2)
    k2 = jnp.tile(jnp.swapaxes(sk[:, 1], 1, 2), (1, 1, PEER_NKEYS))
    kexp = jnp.concatenate([k1, k2], axis=1)
    ub = expert_u[0].astype(bf16)
    vb = expert_v[0].astype(bf16)
    g2, b2 = row(ln2_g[0]), row(ln2_b[0])
    y_p = _peer(h1_p.reshape(B * S, D), h1b_p.reshape(B * S, D), wq, keys, ub, vb, kexp, g2, b2,
                tm_a=PEER_A_TM, tm_b=PEER_B_TM, rt=PEER_B_RT)
    n_s = DB * DS
    y_s = _peer(h1_lead[B:].reshape(n_s, D), h1b_lead[B:].reshape(n_s, D), wq, keys, ub, vb, kexp, g2, b2,
                tm_a=n_s, tm_b=n_s, rt=n_s)

    return (y_p.reshape(B, S, D), y_s.reshape(DB, DS, D),
            pool_p[:, POOL_TAIL - POOL_BUF:][None], s_p[None],
            pool_lead[B:, POOL_TAIL - POOL_BUF:][None], s_lead[B:][None])
```
